```python
import jax, jax.numpy as jnp
from jax import lax
import numpy as np

D_MODEL = 1024
BATCH = 8
SEQ = 2048
DEPTH = 2

GRID_W = 64
CTX_LEN = 256
HEAD_DIM = 64
D_MIX = D_MODEL
LRU_WIDTH = D_MIX // 2
LRU_HEADS = LRU_WIDTH // HEAD_DIM
SC_WIDTH = D_MIX // 4
SC_GROUPS = SC_WIDTH // HEAD_DIM
CM_WIDTH = D_MIX // 4
CM_HEADS = CM_WIDTH // HEAD_DIM
CHUNK = 128
LRU_CONV = 4
LRU_CONV_LEFT = 2
SC_CONV = 3
SC_CONV_LEFT = 1
LRU_C = 8.0
D_FF = 2816
N_SUB = 3
EPS = 1e-6
D_IN = 2 * LRU_WIDTH + 3 * SC_WIDTH + 2 * CM_WIDTH
IN_OFFSETS = (LRU_WIDTH, 2 * LRU_WIDTH, 2 * LRU_WIDTH + SC_WIDTH, 2 * LRU_WIDTH + 2 * SC_WIDTH,
              2 * LRU_WIDTH + 3 * SC_WIDTH, 2 * LRU_WIDTH + 3 * SC_WIDTH + CM_WIDTH)

kernel_name = "hybrid_rglru_shortconv_chunkmlp_dit"


def rms_norm(x, g):
    xf = x.astype(jnp.float32)
    y = xf * lax.rsqrt(jnp.mean(xf * xf, axis=-1, keepdims=True) + EPS)
    return (y * g.astype(jnp.float32)).astype(x.dtype)


def sandwich(x, fn, mod3, g_pre, g_post, weight):
    shift, scale, gate = mod3
    h = rms_norm(x, g_pre) * (1 + scale) + shift
    return x + weight * gate * rms_norm(fn(h), g_post)


def swiglu(h, w_in, w_out):
    g, u = jnp.split(h @ w_in, 2, axis=-1)
    return (jax.nn.silu(g) * u) @ w_out


def dwconv(x, w, b, left):
    K = w.shape[0]
    L = x.shape[1]
    xp = jnp.pad(x, ((0, 0), (left, K - 1 - left), (0, 0)))
    y = b + w[0] * xp[:, 0:L]
    for k in range(1, K):
        y = y + w[k] * xp[:, k:k + L]
    return y


def seq_conv3(z, w, b):
    return dwconv(z, w, b, SC_CONV_LEFT)


def row_conv3(z, w, b):
    B, L, C = z.shape
    rows = L // GRID_W
    return dwconv(z.reshape(B * rows, GRID_W, C), w, b, SC_CONV_LEFT).reshape(B, L, C)


def linear_scan(a, b, h0):
    b = b.at[:, 0].add(a[:, 0] * h0)
    def combine(l, r):
        return (l[0] * r[0], r[0] * l[1] + r[1])
    return lax.associative_scan(combine, (a, b), axis=1)[1]


def lru_coeffs(xc, w_r, b_r, w_i, b_i, lam):
    B, L, _ = xc.shape
    xf = xc.astype(jnp.float32)
    xh = xf.reshape(B, L, LRU_HEADS, HEAD_DIM)
    r = jax.nn.sigmoid(jnp.einsum('blhi,hij->blhj', xh, w_r.astype(jnp.float32)).reshape(B, L, LRU_WIDTH) + b_r)
    i = jax.nn.sigmoid(jnp.einsum('blhi,hij->blhj', xh, w_i.astype(jnp.float32)).reshape(B, L, LRU_WIDTH) + b_i)
    log_a = -LRU_C * r * jax.nn.softplus(-lam.astype(jnp.float32))
    a = jnp.exp(log_a)
    bx = jnp.sqrt(-jnp.expm1(2.0 * log_a)) * (i * xf)
    return a, bx


def bidir_rglru(xc_ctx, xc_lat, w_r, b_r, w_i, b_i, lam, need_ctx_out):
    y_ctx = None
    y_lat = None
    h0 = jnp.zeros((xc_lat.shape[0], LRU_WIDTH), jnp.float32)
    for d in range(2):
        a_c, b_c = lru_coeffs(xc_ctx, w_r[d], b_r[d], w_i[d], b_i[d], lam[d])
        a_l, b_l = lru_coeffs(xc_lat, w_r[d], b_r[d], w_i[d], b_i[d], lam[d])
        if d == 1:
            a_c, b_c, a_l, b_l = (jnp.flip(t, axis=1) for t in (a_c, b_c, a_l, b_l))
        h_c = linear_scan(a_c, b_c, h0)
        h_l = linear_scan(a_l, b_l, h_c[:, -1])
        if d == 1:
            h_c, h_l = jnp.flip(h_c, axis=1), jnp.flip(h_l, axis=1)
        y_lat = h_l if y_lat is None else y_lat + h_l
        if need_ctx_out:
            y_ctx = h_c if y_ctx is None else y_ctx + h_c
    return y_ctx, y_lat


def chunk_mix(u, v, ws, bs):
    B, L, C = v.shape
    n = L // CHUNK
    vh = v.reshape(B, n, CHUNK, CM_HEADS, HEAD_DIM)
    s = jnp.einsum('hpq,bnqhd->bnphd', ws, vh) + bs.T[None, None, :, :, None]
    return u * s.reshape(B, L, C)


def head_groups(parts, y_lru, sc_w, sc_b, cm_w, cm_b, conv_fn):
    _, lru_gate, sc_bg, sc_cg, sc_x, cm_u, cm_v = parts
    o_lru = jax.nn.gelu(lru_gate) * y_lru.astype(lru_gate.dtype)
    o_sc = sc_bg * conv_fn(sc_cg * sc_x, sc_w, sc_b)
    o_cm = chunk_mix(jax.nn.gelu(cm_u), jax.nn.gelu(cm_v), cm_w, cm_b)
    return jnp.concatenate([o_lru, o_sc, o_cm], axis=-1)


def setup_inputs(seed: int = 0) -> dict:
    key = jax.random.key(seed)
    ks = jax.random.split(key, 24)
    f32 = jnp.float32

    def nrm(k, shape, scale):
        return jax.random.normal(k, shape, f32) * scale

    u = jax.random.uniform(ks[17], (DEPTH, 2, LRU_WIDTH), f32, minval=0.9, maxval=0.999)
    s = u ** (1.0 / LRU_C)
    return {
        "x": nrm(ks[0], (BATCH, SEQ, D_MODEL), 1.0),
        "c": nrm(ks[1], (BATCH, D_MODEL), 1.0),
        "ctx": nrm(ks[2], (BATCH, CTX_LEN, D_MODEL), 1.0),
        "c_ctx": nrm(ks[3], (D_MODEL,), 1.0),
        "w_mod": nrm(ks[4], (DEPTH, D_MODEL, 3 * N_SUB * D_MODEL), 0.5 * D_MODEL ** -0.5),
        "b_mod": nrm(ks[5], (DEPTH, 3 * N_SUB * D_MODEL), 0.01),
        "norm_pre": 1.0 + nrm(ks[6], (DEPTH, N_SUB, D_MODEL), 0.05),
        "norm_post": 1.0 + nrm(ks[7], (DEPTH, N_SUB, D_MODEL), 0.05),
        "ffn_w_in": nrm(ks[8], (DEPTH, 2, D_MODEL, 2 * D_FF), D_MODEL ** -0.5),
        "ffn_w_out": nrm(ks[9], (DEPTH, 2, D_FF, D_MODEL), D_FF ** -0.5),
        "w_mix_in": nrm(ks[10], (DEPTH, D_MODEL, D_IN), D_MODEL ** -0.5),
        "w_mix_out": nrm(ks[11], (DEPTH, D_MIX, D_MODEL), D_MIX ** -0.5),
        "lru_conv_w": nrm(ks[12], (DEPTH, LRU_CONV, LRU_WIDTH), LRU_CONV ** -0.5),
        "lru_conv_b": nrm(ks[13], (DEPTH, LRU_WIDTH), 0.01),
        "lru_w_r": nrm(ks[14], (DEPTH, 2, LRU_HEADS, HEAD_DIM, HEAD_DIM), HEAD_DIM ** -0.5),
        "lru_b_r": nrm(ks[15], (DEPTH, 2, LRU_WIDTH), 0.01),
        "lru_w_i": nrm(ks[16], (DEPTH, 2, LRU_HEADS, HEAD_DIM, HEAD_DIM), HEAD_DIM ** -0.5),
        "lru_b_i": nrm(ks[18], (DEPTH, 2, LRU_WIDTH), 0.01),
        "lru_lambda": jnp.log(s) - jnp.log1p(-s),
        "sc_conv_w": nrm(ks[19], (DEPTH, SC_CONV, SC_WIDTH), SC_CONV ** -0.5),
        "sc_conv_b": nrm(ks[20], (DEPTH, SC_WIDTH), 0.01),
        "cm_w_s": nrm(ks[21], (DEPTH, CM_HEADS, CHUNK, CHUNK), CHUNK ** -0.5),
        "cm_b_s": 1.0 + nrm(ks[22], (DEPTH, CM_HEADS, CHUNK), 0.05),
    }


def reference(x, c, ctx, c_ctx, w_mod, b_mod, norm_pre, norm_post, ffn_w_in, ffn_w_out,
              w_mix_in, w_mix_out, lru_conv_w, lru_conv_b, lru_w_r, lru_b_r, lru_w_i, lru_b_i,
              lru_lambda, sc_conv_w, sc_conv_b, cm_w_s, cm_b_s):
    xl, xc = x, ctx
    for l in range(DEPTH):
        last = l == DEPTH - 1
        mod_l = (jax.nn.silu(c) @ w_mod[l] + b_mod[l])[:, None, :]
        mod_c = (jax.nn.silu(c_ctx) @ w_mod[l] + b_mod[l])[None, None, :]
        ml = jnp.split(mod_l, 3 * N_SUB, axis=-1)
        mc = jnp.split(mod_c, 3 * N_SUB, axis=-1)

        ffn0 = lambda h, l=l: swiglu(h, ffn_w_in[l, 0], ffn_w_out[l, 0])
        xl = sandwich(xl, ffn0, ml[0:3], norm_pre[l, 0], norm_post[l, 0], 0.5)
        xc = sandwich(xc, ffn0, mc[0:3], norm_pre[l, 0], norm_post[l, 0], 0.5)

        hl = rms_norm(xl, norm_pre[l, 1]) * (1 + ml[4]) + ml[3]
        hc = rms_norm(xc, norm_pre[l, 1]) * (1 + mc[4]) + mc[3]
        pl = jnp.split(hl @ w_mix_in[l], IN_OFFSETS, axis=-1)
        xl_lru = dwconv(pl[0], lru_conv_w[l], lru_conv_b[l], LRU_CONV_LEFT)
        if last:
            pc = None
            xc_lru = dwconv(hc @ w_mix_in[l][:, :LRU_WIDTH], lru_conv_w[l], lru_conv_b[l], LRU_CONV_LEFT)
        else:
            pc = jnp.split(hc @ w_mix_in[l], IN_OFFSETS, axis=-1)
            xc_lru = dwconv(pc[0], lru_conv_w[l], lru_conv_b[l], LRU_CONV_LEFT)
        y_c, y_l = bidir_rglru(xc_lru, xl_lru, lru_w_r[l], lru_b_r[l], lru_w_i[l], lru_b_i[l],
                               lru_lambda[l], not last)
        mix_l = head_groups(pl, y_l, sc_conv_w[l], sc_conv_b[l], cm_w_s[l], cm_b_s[l], row_conv3) @ w_mix_out[l]
        xl = xl + ml[5] * rms_norm(mix_l, norm_post[l, 1])
        if not last:
            mix_c = head_groups(pc, y_c, sc_conv_w[l], sc_conv_b[l], cm_w_s[l], cm_b_s[l], seq_conv3) @ w_mix_out[l]
            xc = xc + mc[5] * rms_norm(mix_c, norm_post[l, 1])

        ffn1 = lambda h, l=l: swiglu(h, ffn_w_in[l, 1], ffn_w_out[l, 1])
        xl = sandwich(xl, ffn1, ml[6:9], norm_pre[l, 2], norm_post[l, 2], 0.5)
        if not last:
            xc = sandwich(xc, ffn1, mc[6:9], norm_pre[l, 2], norm_post[l, 2], 0.5)
    return xl
```

```python
import functools

import jax
import jax.numpy as jnp
from jax import lax
from jax.experimental import pallas as pl
from jax.experimental.pallas import tpu as pltpu

D_MODEL = 1024
BATCH = 8
SEQ = 2048
DEPTH = 2
GRID_W = 64
CTX_LEN = 256
HEAD_DIM = 64
LRU_WIDTH = 512
SC_WIDTH = 256
CM_WIDTH = 256
CM_HEADS = 4
CHUNK = 128
LRU_CONV = 4
LRU_CONV_LEFT = 2
LRU_C = 8.0
D_FF = 2816
N_SUB = 3
EPS = 1e-6
D_IN = 2 * LRU_WIDTH + 3 * SC_WIDTH + 2 * CM_WIDTH
D_REST = D_IN - LRU_WIDTH
TOTAL = CTX_LEN + SEQ
N_MOD = 3 * N_SUB

TM = 256
N_TILES = TOTAL // TM
MOD_ROWS = 16
SUBLANES = 8
PAD = SUBLANES
VMEM_LIMIT = 56 * 1024 * 1024

F32 = jnp.float32
BF16 = jnp.bfloat16


def _rms(x, g):
    return x * lax.rsqrt(jnp.mean(x * x, axis=-1, keepdims=True) + EPS) * g


def _dot(a, b):
    return jnp.dot(a, b, preferred_element_type=F32)


def _gelu(x):
    return jax.nn.gelu(x, approximate=True)


def _mod_kernel(c_ref, w_ref, b_ref, o_ref):
    s = c_ref[...]
    s = s * jax.nn.sigmoid(s)
    o_ref[0] = _dot(s.astype(BF16), w_ref[0].astype(BF16)) + b_ref[0]


def _modulation(c_all, w_mod, b_mod):
    tn = 2304
    n_out = N_MOD * D_MODEL
    return pl.pallas_call(
        _mod_kernel,
        grid=(DEPTH, n_out // tn),
        in_specs=[
            pl.BlockSpec((MOD_ROWS, D_MODEL), lambda l, n: (0, 0)),
            pl.BlockSpec((1, D_MODEL, tn), lambda l, n: (l, 0, n)),
            pl.BlockSpec((1, 1, tn), lambda l, n: (l, 0, n)),
        ],
        out_specs=pl.BlockSpec((1, MOD_ROWS, tn), lambda l, n: (l, 0, n)),
        out_shape=jax.ShapeDtypeStruct((DEPTH, MOD_ROWS, n_out), F32),
        compiler_params=pltpu.CompilerParams(
            dimension_semantics=("arbitrary", "arbitrary"), vmem_limit_bytes=VMEM_LIMIT),
        name="adaln_mod",
    )(c_all, w_mod, b_mod.reshape(DEPTH, 1, n_out))


def _mod_row_index(layer, b, j, first_tile):
    return layer * MOD_ROWS + jnp.where(j + first_tile == 0, BATCH, b)


def _tile_offset(rows):
    assert rows in (TOTAL, SEQ)
    return (TOTAL - rows) // TM


def _ffn_kernel(x_ref, mod_ref, gpre_ref, gpost_ref, win_ref, wout_ref, o_ref, *, layer, sub):
    x = x_ref[0]
    shift = mod_ref[0, 3 * sub:3 * sub + 1, :]
    scale = mod_ref[0, 3 * sub + 1:3 * sub + 2, :]
    gate = mod_ref[0, 3 * sub + 2:3 * sub + 3, :]
    row = layer * N_SUB + sub
    h = _rms(x, gpre_ref[row:row + 1, :]) * (1.0 + scale) + shift
    hb = h.astype(BF16)
    g = _dot(hb, win_ref[:, :D_FF])
    u = _dot(hb, win_ref[:, D_FF:])
    act = (g * jax.nn.sigmoid(g) * u).astype(BF16)
    f = _dot(act, wout_ref[...])
    o_ref[0] = x + 0.5 * gate * _rms(f, gpost_ref[row:row + 1, :])


def _ffn(xs, mod, gpre, gpost, w_in, w_out, *, layer, sub, which, first_tile, out_rows):
    n_tiles = N_TILES - first_tile
    in_first = _tile_offset(xs.shape[1])
    out_first = _tile_offset(out_rows)
    kern = functools.partial(_ffn_kernel, layer=layer, sub=sub)
    return pl.pallas_call(
        kern,
        grid=(BATCH, n_tiles),
        in_specs=[
            pl.BlockSpec((1, TM, D_MODEL), lambda b, j: (b, j + first_tile - in_first, 0)),
            pl.BlockSpec((1, N_MOD, D_MODEL),
                         lambda b, j: (_mod_row_index(layer, b, j, first_tile), 0, 0)),
            pl.BlockSpec((DEPTH * N_SUB, D_MODEL), lambda b, j: (0, 0)),
            pl.BlockSpec((DEPTH * N_SUB, D_MODEL), lambda b, j: (0, 0)),
            pl.BlockSpec((None, None, D_MODEL, 2 * D_FF), lambda b, j: (layer, which, 0, 0),
                         pipeline_mode=pl.Buffered(1)),
            pl.BlockSpec((None, None, D_FF, D_MODEL), lambda b, j: (layer, which, 0, 0),
                         pipeline_mode=pl.Buffered(1)),
        ],
        out_specs=pl.BlockSpec((1, TM, D_MODEL), lambda b, j: (b, j + first_tile - out_first, 0)),
        out_shape=jax.ShapeDtypeStruct((BATCH, out_rows, D_MODEL), F32),
        compiler_params=pltpu.CompilerParams(
            dimension_semantics=("arbitrary", "arbitrary"), vmem_limit_bytes=VMEM_LIMIT),
        name=f"ffn_l{layer}_s{sub}",
    )(xs, mod, gpre, gpost, w_in, w_out)


CTX_BASE = PAD
LAT_BASE = PAD + CTX_LEN + PAD
PAD_ROWS = LAT_BASE + SEQ + PAD
N_BLK = TOTAL // SUBLANES
CTX_BLK = CTX_LEN // SUBLANES


def _pad_base(i):
    return jnp.where(i == 0, CTX_BASE, LAT_BASE - CTX_LEN) + i * TM


def _block_scan(a, b, reverse):
    rows = lax.broadcasted_iota(jnp.int32, a.shape, 0)
    for k in (1, 2, 4):
        if reverse:
            valid = rows < SUBLANES - k
            shift = SUBLANES - k
        else:
            valid = rows >= k
            shift = k
        a_prev = jnp.where(valid, pltpu.roll(a, shift, axis=0), 1.0)
        b_prev = jnp.where(valid, pltpu.roll(b, shift, axis=0), 0.0)
        b = b + a * b_prev
        a = a * a_prev
    return a, b


def _lru_kernel(x_ref, modl_ref, modc_ref, gpre_ref, w0_ref, cw_ref, cb_ref, wg_ref,
                br_ref, bi_ref, lam_ref, y_ref, p0_ref, a_ref, b_ref, *, layer):
    row = layer * N_SUB + 1
    gpre = gpre_ref[row:row + 1, :]

    zeros = jnp.zeros((PAD, LRU_WIDTH), F32)
    p0_ref[0:PAD, :] = zeros
    p0_ref[LAT_BASE - PAD:LAT_BASE, :] = zeros
    p0_ref[PAD_ROWS - PAD:PAD_ROWS, :] = zeros

    def project(i, mod_ref):
        x = x_ref[0, pl.ds(pl.multiple_of(i * TM, TM), TM), :]
        h = _rms(x, gpre) * (1.0 + mod_ref[0, 4:5, :]) + mod_ref[0, 3:4, :]
        dst = pl.multiple_of(_pad_base(i), SUBLANES)
        p0_ref[pl.ds(dst, TM), :] = _dot(h.astype(BF16), w0_ref[...])

    project(0, modc_ref)

    def project_body(i, carry):
        project(i, modl_ref)
        return carry

    lax.fori_loop(1, N_TILES, project_body, 0)

    cb = cb_ref[layer:layer + 1, :]

    for d in range(2):
        lam = lam_ref[layer, d:d + 1, :]
        neg_c_softplus = -LRU_C * (jnp.maximum(-lam, 0.0) + jnp.log1p(jnp.exp(-jnp.abs(lam))))
        b_r = br_ref[layer, d:d + 1, :]
        b_i = bi_ref[layer, d:d + 1, :]

        def coeff_body(i, carry, d=d, neg_c_softplus=neg_c_softplus, b_r=b_r, b_i=b_i):
            src = pl.multiple_of(_pad_base(i) - PAD, SUBLANES)
            window = p0_ref[pl.ds(src, TM + 2 * PAD), :]
            xc = cb
            for k in range(LRU_CONV):
                start = PAD + k - LRU_CONV_LEFT
                xc = xc + cw_ref[layer, k:k + 1, :] * window[start:start + TM, :]
            dst = pl.multiple_of(i * TM, TM)
            half_w = LRU_WIDTH // 2
            for hh in range(2):
                cols = slice(hh * half_w, (hh + 1) * half_w)
                xh = xc[:, cols]
                g = _dot(xh.astype(BF16), wg_ref[hh, :, d * LRU_WIDTH:(d + 1) * LRU_WIDTH])
                r = jax.nn.sigmoid(g[:, :half_w] + b_r[:, cols])
                gi = jax.nn.sigmoid(g[:, half_w:] + b_i[:, cols])
                a = jnp.exp(r * neg_c_softplus[:, cols])
                a_ref[pl.ds(dst, TM), cols] = a
                b_ref[pl.ds(dst, TM), cols] = jnp.sqrt(1.0 - a * a) * (gi * xh)
            return carry

        lax.fori_loop(0, N_TILES, coeff_body, 0)

        def scan_body(s, c, d=d):
            if d == 0:
                blk = s
            else:
                blk = jnp.where(s < CTX_BLK, CTX_BLK - 1 - s, N_BLK + CTX_BLK - 1 - s)
            r0 = pl.multiple_of(blk * SUBLANES, SUBLANES)
            a_cum, b_loc = _block_scan(a_ref[pl.ds(r0, SUBLANES), :],
                                       b_ref[pl.ds(r0, SUBLANES), :], reverse=(d == 1))
            h = b_loc + a_cum * c
            if d == 0:
                y_ref[0, pl.ds(r0, SUBLANES), :] = h
                last = SUBLANES - 1
            else:
                y_ref[0, pl.ds(r0, SUBLANES), :] += h
                last = 0
            return b_loc[last:last + 1, :] + a_cum[last:last + 1, :] * c

        lax.fori_loop(0, N_BLK, scan_body, jnp.zeros((1, LRU_WIDTH), F32), unroll=4)


def _lru(xs, mod, gpre, w_mix_in, conv_w, conv_b, w_gate, b_r, b_i, lam, *, layer):
    kern = functools.partial(_lru_kernel, layer=layer)
    full = lambda shape: pl.BlockSpec(shape, lambda b: (0,) * len(shape))
    return pl.pallas_call(
        kern,
        grid=(BATCH,),
        in_specs=[
            pl.BlockSpec((1, TOTAL, D_MODEL), lambda b: (b, 0, 0)),
            pl.BlockSpec((1, N_MOD, D_MODEL), lambda b: (layer * MOD_ROWS + b, 0, 0)),
            pl.BlockSpec((1, N_MOD, D_MODEL), lambda b: (layer * MOD_ROWS + BATCH, 0, 0)),
            full((DEPTH * N_SUB, D_MODEL)),
            pl.BlockSpec((None, D_MODEL, LRU_WIDTH), lambda b: (layer, 0, 0)),
            full((DEPTH, LRU_CONV, LRU_WIDTH)),
            full((DEPTH, LRU_WIDTH)),
            pl.BlockSpec((None, 2, LRU_WIDTH // 2, 2 * LRU_WIDTH), lambda b: (layer, 0, 0, 0)),
            full((DEPTH, 2, LRU_WIDTH)),
            full((DEPTH, 2, LRU_WIDTH)),
            full((DEPTH, 2, LRU_WIDTH)),
        ],
        out_specs=pl.BlockSpec((1, TOTAL, LRU_WIDTH), lambda b: (b, 0, 0)),
        out_shape=jax.ShapeDtypeStruct((BATCH, TOTAL, LRU_WIDTH), F32),
        scratch_shapes=[
            pltpu.VMEM((PAD_ROWS, LRU_WIDTH), F32),
            pltpu.VMEM((TOTAL, LRU_WIDTH), F32),
            pltpu.VMEM((TOTAL, LRU_WIDTH), F32),
        ],
        compiler_params=pltpu.CompilerParams(
            dimension_semantics=("arbitrary",), vmem_limit_bytes=VMEM_LIMIT),
        name=f"lru_l{layer}",
    )(xs, mod, mod, gpre, w_mix_in, conv_w, conv_b, w_gate, b_r, b_i, lam)


def _heads_kernel(x_ref, mod_ref, gpre_ref, gpost_ref, win_ref, y_ref, scw_ref, scb_ref,
                  cmw_ref, cmb_ref, wout_ref, o_ref, *, layer, first_tile):
    x = x_ref[0]
    row = layer * N_SUB + 1
    h = _rms(x, gpre_ref[row:row + 1, :]) * (1.0 + mod_ref[0, 4:5, :]) + mod_ref[0, 3:4, :]
    p = _dot(h.astype(BF16), win_ref[...])

    o_lru = _gelu(p[:, 0:LRU_WIDTH]) * y_ref[0]

    c0 = LRU_WIDTH
    sc_bg = p[:, c0:c0 + SC_WIDTH]
    z = p[:, c0 + SC_WIDTH:c0 + 2 * SC_WIDTH] * p[:, c0 + 2 * SC_WIDTH:c0 + 3 * SC_WIDTH]
    is_ctx = pl.program_id(1) + first_tile == 0
    period_mask = jnp.where(is_ctx, CTX_LEN - 1, GRID_W - 1)
    pos = lax.broadcasted_iota(jnp.int32, (TM, SC_WIDTH), 0) & period_mask
    z_prev = jnp.where(pos != 0, pltpu.roll(z, 1, axis=0), 0.0)
    z_next = jnp.where(pos != period_mask, pltpu.roll(z, TM - 1, axis=0), 0.0)
    conv = (scb_ref[layer:layer + 1, :] + scw_ref[layer, 0:1, :] * z_prev
            + scw_ref[layer, 1:2, :] * z + scw_ref[layer, 2:3, :] * z_next)
    o_sc = sc_bg * conv

    c1 = LRU_WIDTH + 3 * SC_WIDTH
    gu = _gelu(p[:, c1:c1 + CM_WIDTH])
    gv = _gelu(p[:, c1 + CM_WIDTH:c1 + 2 * CM_WIDTH])
    lane_head = lax.broadcasted_iota(jnp.int32, (CHUNK, CM_WIDTH), 1) >> (HEAD_DIM.bit_length() - 1)
    o_cm = []
    for n in range(TM // CHUNK):
        gvn = gv[n * CHUNK:(n + 1) * CHUNK, :]
        stacked = jnp.concatenate(
            [jnp.where(lane_head == hd, gvn, 0.0) for hd in range(CM_HEADS)], axis=0)
        s = _dot(cmw_ref[...], stacked.astype(BF16)) + cmb_ref[...]
        o_cm.append(gu[n * CHUNK:(n + 1) * CHUNK, :] * s)
    o_cm = jnp.concatenate(o_cm, axis=0)

    mix = (_dot(o_lru.astype(BF16), wout_ref[0:LRU_WIDTH, :])
           + _dot(o_sc.astype(BF16), wout_ref[LRU_WIDTH:LRU_WIDTH + SC_WIDTH, :])
           + _dot(o_cm.astype(BF16), wout_ref[LRU_WIDTH + SC_WIDTH:, :]))
    o_ref[0] = x + mod_ref[0, 5:6, :] * _rms(mix, gpost_ref[row:row + 1, :])


def _heads(xs, mod, gpre, gpost, w_mix_in, y, sc_w, sc_b, cm_w, cm_b, w_mix_out, *, layer,
           first_tile, out_rows):
    n_tiles = N_TILES - first_tile
    out_first = _tile_offset(out_rows)
    kern = functools.partial(_heads_kernel, layer=layer, first_tile=first_tile)
    full = lambda shape: pl.BlockSpec(shape, lambda b, j: (0,) * len(shape))
    return pl.pallas_call(
        kern,
        grid=(BATCH, n_tiles),
        in_specs=[
            pl.BlockSpec((1, TM, D_MODEL), lambda b, j: (b, j + first_tile, 0)),
            pl.BlockSpec((1, N_MOD, D_MODEL),
                         lambda b, j: (_mod_row_index(layer, b, j, first_tile), 0, 0)),
            full((DEPTH * N_SUB, D_MODEL)),
            full((DEPTH * N_SUB, D_MODEL)),
            pl.BlockSpec((None, D_MODEL, D_REST), lambda b, j: (layer, 0, 0)),
            pl.BlockSpec((1, TM, LRU_WIDTH), lambda b, j: (b, j + first_tile, 0)),
            full((DEPTH, 3, SC_WIDTH)),
            full((DEPTH, SC_WIDTH)),
            pl.BlockSpec((None, CHUNK, CM_HEADS * CHUNK), lambda b, j: (layer, 0, 0)),
            pl.BlockSpec((None, CHUNK, CM_WIDTH), lambda b, j: (layer, 0, 0)),
            pl.BlockSpec((None, D_MODEL, D_MODEL), lambda b, j: (layer, 0, 0)),
        ],
        out_specs=pl.BlockSpec((1, TM, D_MODEL), lambda b, j: (b, j + first_tile - out_first, 0)),
        out_shape=jax.ShapeDtypeStruct((BATCH, out_rows, D_MODEL), F32),
        compiler_params=pltpu.CompilerParams(
            dimension_semantics=("arbitrary", "arbitrary"), vmem_limit_bytes=VMEM_LIMIT),
        name=f"heads_l{layer}",
    )(xs, mod, gpre, gpost, w_mix_in, y, sc_w, sc_b, cm_w, cm_b, w_mix_out)


def _gate_weights(w_r, w_i):
    heads_per_half = (LRU_WIDTH // 2) // HEAD_DIM
    eye = jnp.eye(heads_per_half, dtype=F32)

    def block_diag(w):
        w = w.reshape(DEPTH, 2, 2, heads_per_half, HEAD_DIM, HEAD_DIM)
        bd = jnp.einsum('ldpgij,gk->ldpgikj', w, eye)
        return bd.reshape(DEPTH, 2, 2, LRU_WIDTH // 2, LRU_WIDTH // 2)

    r, i = block_diag(w_r), block_diag(w_i)
    cols = [r[:, 0], i[:, 0], r[:, 1], i[:, 1]]
    return jnp.concatenate(cols, axis=-1).astype(BF16)


def kernel(x, c, ctx, c_ctx, w_mod, b_mod, norm_pre, norm_post, ffn_w_in, ffn_w_out, w_mix_in,
           w_mix_out, lru_conv_w, lru_conv_b, lru_w_r, lru_b_r, lru_w_i, lru_b_i, lru_lambda,
           sc_conv_w, sc_conv_b, cm_w_s, cm_b_s):
    c_all = jnp.concatenate(
        [c, c_ctx[None, :], jnp.zeros((MOD_ROWS - BATCH - 1, D_MODEL), F32)], axis=0)
    mod = _modulation(c_all, w_mod, b_mod).reshape(DEPTH * MOD_ROWS, N_MOD, D_MODEL)

    gpre = norm_pre.reshape(DEPTH * N_SUB, D_MODEL)
    gpost = norm_post.reshape(DEPTH * N_SUB, D_MODEL)
    w_in = ffn_w_in.astype(BF16)
    w_out = ffn_w_out.astype(BF16)
    w_mi_lru = w_mix_in[:, :, :LRU_WIDTH].astype(BF16)
    w_mi_rest = w_mix_in[:, :, LRU_WIDTH:].astype(BF16)
    w_mo = w_mix_out.astype(BF16)
    w_gate = _gate_weights(lru_w_r, lru_w_i)
    cm_w = cm_w_s.transpose(0, 2, 1, 3).reshape(DEPTH, CHUNK, CM_HEADS * CHUNK).astype(BF16)
    cm_b = jnp.repeat(cm_b_s.transpose(0, 2, 1), HEAD_DIM, axis=2)

    xs = jnp.concatenate([ctx, x], axis=1)
    for layer in range(DEPTH):
        last = layer == DEPTH - 1
        xs = _ffn(xs, mod, gpre, gpost, w_in, w_out, layer=layer, sub=0, which=0,
                  first_tile=0, out_rows=TOTAL)
        y = _lru(xs, mod, gpre, w_mi_lru, lru_conv_w, lru_conv_b, w_gate, lru_b_r, lru_b_i,
                 lru_lambda, layer=layer)
        first_tile = 1 if last else 0
        out_rows = SEQ if last else TOTAL
        xs = _heads(xs, mod, gpre, gpost, w_mi_rest, y, sc_conv_w, sc_conv_b, cm_w, cm_b, w_mo,
                    layer=layer, first_tile=first_tile, out_rows=out_rows)
        xs = _ffn(xs, mod, gpre, gpost, w_in, w_out, layer=layer, sub=2, which=1,
                  first_tile=first_tile, out_rows=out_rows)
    return xs
```

```python
import functools

import jax
import jax.numpy as jnp
from jax import lax
from jax.experimental import pallas as pl
from jax.experimental.pallas import tpu as pltpu

D_MODEL = 1024
BATCH = 8
SEQ = 2048
DEPTH = 2
GRID_W = 64
CTX_LEN = 256
HEAD_DIM = 64
LRU_WIDTH = 512
SC_WIDTH = 256
CM_WIDTH = 256
CM_HEADS = 4
CHUNK = 128
LRU_CONV = 4
LRU_CONV_LEFT = 2
LRU_C = 8.0
D_FF = 2816
N_SUB = 3
EPS = 1e-6
D_IN = 2 * LRU_WIDTH + 3 * SC_WIDTH + 2 * CM_WIDTH
D_REST = D_IN - LRU_WIDTH
TOTAL = CTX_LEN + SEQ
N_MOD = 3 * N_SUB

SUBT = 256
ROW_TILE = 512
MOD_ROWS = 16
SUBLANES = 8
PAD = SUBLANES
VMEM_LIMIT = 56 * 1024 * 1024

F32 = jnp.float32
BF16 = jnp.bfloat16


def _rms(x, g):
    return x * lax.rsqrt(jnp.mean(x * x, axis=-1, keepdims=True) + EPS) * g


def _dot(a, b):
    return jnp.dot(a, b, preferred_element_type=F32)


def _gelu(x):
    return jax.nn.gelu(x, approximate=True)


def _full(shape):
    return pl.BlockSpec(shape, lambda *_: (0,) * len(shape))


def _mod_spec(layer, rows_per_batch):
    if rows_per_batch is None:
        return pl.BlockSpec((1, N_MOD, D_MODEL), lambda i: (layer * MOD_ROWS + BATCH, 0, 0))
    tiles_per_batch = rows_per_batch // ROW_TILE
    return pl.BlockSpec((1, N_MOD, D_MODEL),
                        lambda i: (layer * MOD_ROWS + i // tiles_per_batch, 0, 0))


def _mod_kernel(c_ref, w_ref, b_ref, o_ref):
    s = c_ref[...]
    s = s * jax.nn.sigmoid(s)
    o_ref[0] = _dot(s.astype(BF16), w_ref[0].astype(BF16)) + b_ref[0]


def _modulation(c_all, w_mod, b_mod):
    tn = 2304
    n_out = N_MOD * D_MODEL
    return pl.pallas_call(
        _mod_kernel,
        grid=(DEPTH, n_out // tn),
        in_specs=[
            pl.BlockSpec((MOD_ROWS, D_MODEL), lambda l, n: (0, 0)),
            pl.BlockSpec((1, D_MODEL, tn), lambda l, n: (l, 0, n)),
            pl.BlockSpec((1, 1, tn), lambda l, n: (l, 0, n)),
        ],
        out_specs=pl.BlockSpec((1, MOD_ROWS, tn), lambda l, n: (l, 0, n)),
        out_shape=jax.ShapeDtypeStruct((DEPTH, MOD_ROWS, n_out), F32),
        compiler_params=pltpu.CompilerParams(
            dimension_semantics=("arbitrary", "arbitrary"), vmem_limit_bytes=VMEM_LIMIT),
        name="adaln_mod",
    )(c_all, w_mod, b_mod.reshape(DEPTH, 1, n_out))


def _ffn_kernel(x_ref, mod_ref, gpre_ref, gpost_ref, win_ref, wout_ref, o_ref, *, layer, sub):
    shift = mod_ref[0, 3 * sub:3 * sub + 1, :]
    scale = mod_ref[0, 3 * sub + 1:3 * sub + 2, :]
    gate = mod_ref[0, 3 * sub + 2:3 * sub + 3, :]
    row = layer * N_SUB + sub
    for s in range(ROW_TILE // SUBT):
        rows = pl.ds(s * SUBT, SUBT)
        x = x_ref[rows, :]
        h = _rms(x, gpre_ref[row:row + 1, :]) * (1.0 + scale) + shift
        hb = h.astype(BF16)
        g = _dot(hb, win_ref[:, :D_FF])
        u = _dot(hb, win_ref[:, D_FF:])
        act = (g * jax.nn.sigmoid(g) * u).astype(BF16)
        f = _dot(act, wout_ref[...])
        o_ref[rows, :] = x + 0.5 * gate * _rms(f, gpost_ref[row:row + 1, :])


def _ffn(x, mod, gpre, gpost, w_in, w_out, *, layer, sub, which, rows_per_batch):
    n_rows = x.shape[0]
    kern = functools.partial(_ffn_kernel, layer=layer, sub=sub)
    return pl.pallas_call(
        kern,
        grid=(n_rows // ROW_TILE,),
        in_specs=[
            pl.BlockSpec((ROW_TILE, D_MODEL), lambda i: (i, 0)),
            _mod_spec(layer, rows_per_batch),
            _full((DEPTH * N_SUB, D_MODEL)),
            _full((DEPTH * N_SUB, D_MODEL)),
            pl.BlockSpec((None, None, D_MODEL, 2 * D_FF), lambda i: (layer, which, 0, 0),
                         pipeline_mode=pl.Buffered(1)),
            pl.BlockSpec((None, None, D_FF, D_MODEL), lambda i: (layer, which, 0, 0),
                         pipeline_mode=pl.Buffered(1)),
        ],
        out_specs=pl.BlockSpec((ROW_TILE, D_MODEL), lambda i: (i, 0)),
        out_shape=jax.ShapeDtypeStruct((n_rows, D_MODEL), F32),
        compiler_params=pltpu.CompilerParams(
            dimension_semantics=("arbitrary",), vmem_limit_bytes=VMEM_LIMIT),
        name=f"ffn_l{layer}_s{sub}_{'ctx' if rows_per_batch is None else 'lat'}",
    )(x, mod, gpre, gpost, w_in, w_out)


CTX_BASE = PAD
LAT_BASE = PAD + CTX_LEN + PAD
PAD_ROWS = LAT_BASE + SEQ + PAD
N_SCAN_TILES = TOTAL // SUBT
CTX_BLK = CTX_LEN // SUBLANES
LAT_BLK = SEQ // SUBLANES


def _pad_base(i):
    return jnp.where(i == 0, CTX_BASE, LAT_BASE - CTX_LEN) + i * SUBT


def _block_scan(a, b, reverse):
    rows = lax.broadcasted_iota(jnp.int32, a.shape, 0)
    for k in (1, 2, 4):
        if reverse:
            valid = rows < SUBLANES - k
            shift = SUBLANES - k
        else:
            valid = rows >= k
            shift = k
        a_prev = jnp.where(valid, pltpu.roll(a, shift, axis=0), 1.0)
        b_prev = jnp.where(valid, pltpu.roll(b, shift, axis=0), 0.0)
        b = b + a * b_prev
        a = a * a_prev
    return a, b


def _lru_kernel(xl_ref, xc_ref, modl_ref, modc_ref, gpre_ref, w0_ref, cw_ref, cb_ref, wg_ref,
                br_ref, bi_ref, lam_ref, yl_ref, yc_ref, p0_ref, a_ref, b_ref, *, layer):
    row = layer * N_SUB + 1
    gpre = gpre_ref[row:row + 1, :]

    zeros = jnp.zeros((PAD, LRU_WIDTH), F32)
    p0_ref[0:PAD, :] = zeros
    p0_ref[LAT_BASE - PAD:LAT_BASE, :] = zeros
    p0_ref[PAD_ROWS - PAD:PAD_ROWS, :] = zeros

    def project(x, mod_ref, dst):
        h = _rms(x, gpre) * (1.0 + mod_ref[0, 4:5, :]) + mod_ref[0, 3:4, :]
        p0_ref[pl.ds(dst, SUBT), :] = _dot(h.astype(BF16), w0_ref[...])

    project(xc_ref[...], modc_ref, CTX_BASE)

    def project_body(i, carry):
        src = pl.multiple_of(i * SUBT, SUBT)
        project(xl_ref[pl.ds(src, SUBT), :], modl_ref, pl.multiple_of(LAT_BASE + src, SUBLANES))
        return carry

    lax.fori_loop(0, SEQ // SUBT, project_body, 0)

    cb = cb_ref[layer:layer + 1, :]

    for d in range(2):
        lam = lam_ref[layer, d:d + 1, :]
        neg_c_softplus = -LRU_C * (jnp.maximum(-lam, 0.0) + jnp.log1p(jnp.exp(-jnp.abs(lam))))
        b_r = br_ref[layer, d:d + 1, :]
        b_i = bi_ref[layer, d:d + 1, :]

        def coeff_body(i, carry, d=d, neg_c_softplus=neg_c_softplus, b_r=b_r, b_i=b_i):
            src = pl.multiple_of(_pad_base(i) - PAD, SUBLANES)
            window = p0_ref[pl.ds(src, SUBT + 2 * PAD), :]
            xc = cb
            for k in range(LRU_CONV):
                start = PAD + k - LRU_CONV_LEFT
                xc = xc + cw_ref[layer, k:k + 1, :] * window[start:start + SUBT, :]
            dst = pl.multiple_of(i * SUBT, SUBT)
            half_w = LRU_WIDTH // 2
            for hh in range(2):
                cols = slice(hh * half_w, (hh + 1) * half_w)
                xh = xc[:, cols]
                g = _dot(xh.astype(BF16), wg_ref[hh, :, d * LRU_WIDTH:(d + 1) * LRU_WIDTH])
                r = jax.nn.sigmoid(g[:, :half_w] + b_r[:, cols])
                gi = jax.nn.sigmoid(g[:, half_w:] + b_i[:, cols])
                a = jnp.exp(r * neg_c_softplus[:, cols])
                a_ref[pl.ds(dst, SUBT), cols] = a
                b_ref[pl.ds(dst, SUBT), cols] = jnp.sqrt(1.0 - a * a) * (gi * xh)
            return carry

        lax.fori_loop(0, N_SCAN_TILES, coeff_body, 0)

        def scan_segment(y_ref, scratch_row0, n_blk, carry, d=d):
            def body(s, c):
                blk = s if d == 0 else n_blk - 1 - s
                r0 = pl.multiple_of(blk * SUBLANES, SUBLANES)
                src = pl.multiple_of(scratch_row0 + r0, SUBLANES)
                a_cum, b_loc = _block_scan(a_ref[pl.ds(src, SUBLANES), :],
                                           b_ref[pl.ds(src, SUBLANES), :], reverse=(d == 1))
                h = b_loc + a_cum * c
                if d == 0:
                    y_ref[pl.ds(r0, SUBLANES), :] = h
                    last = SUBLANES - 1
                else:
                    y_ref[pl.ds(r0, SUBLANES), :] += h
                    last = 0
                return b_loc[last:last + 1, :] + a_cum[last:last + 1, :] * c

            return lax.fori_loop(0, n_blk, body, carry, unroll=4)

        carry = scan_segment(yc_ref, 0, CTX_BLK, jnp.zeros((1, LRU_WIDTH), F32))
        scan_segment(yl_ref, CTX_LEN, LAT_BLK, carry)


def _lru(xl, xc, mod, gpre, w_mix_in, conv_w, conv_b, w_gate, b_r, b_i, lam, *, layer):
    kern = functools.partial(_lru_kernel, layer=layer)
    return pl.pallas_call(
        kern,
        grid=(BATCH,),
        in_specs=[
            pl.BlockSpec((SEQ, D_MODEL), lambda b: (b, 0)),
            pl.BlockSpec((CTX_LEN, D_MODEL), lambda b: (b, 0)),
            pl.BlockSpec((1, N_MOD, D_MODEL), lambda b: (layer * MOD_ROWS + b, 0, 0)),
            pl.BlockSpec((1, N_MOD, D_MODEL), lambda b: (layer * MOD_ROWS + BATCH, 0, 0)),
            _full((DEPTH * N_SUB, D_MODEL)),
            pl.BlockSpec((None, D_MODEL, LRU_WIDTH), lambda b: (layer, 0, 0)),
            _full((DEPTH, LRU_CONV, LRU_WIDTH)),
            _full((DEPTH, LRU_WIDTH)),
            pl.BlockSpec((None, 2, LRU_WIDTH // 2, 2 * LRU_WIDTH), lambda b: (layer, 0, 0, 0)),
            _full((DEPTH, 2, LRU_WIDTH)),
            _full((DEPTH, 2, LRU_WIDTH)),
            _full((DEPTH, 2, LRU_WIDTH)),
        ],
        out_specs=[
            pl.BlockSpec((SEQ, LRU_WIDTH), lambda b: (b, 0)),
            pl.BlockSpec((CTX_LEN, LRU_WIDTH), lambda b: (b, 0)),
        ],
        out_shape=[
            jax.ShapeDtypeStruct((BATCH * SEQ, LRU_WIDTH), F32),
            jax.ShapeDtypeStruct((BATCH * CTX_LEN, LRU_WIDTH), F32),
        ],
        scratch_shapes=[
            pltpu.VMEM((PAD_ROWS, LRU_WIDTH), F32),
            pltpu.VMEM((TOTAL, LRU_WIDTH), F32),
            pltpu.VMEM((TOTAL, LRU_WIDTH), F32),
        ],
        compiler_params=pltpu.CompilerParams(
            dimension_semantics=("arbitrary",), vmem_limit_bytes=VMEM_LIMIT),
        name=f"lru_l{layer}",
    )(xl, xc, mod, mod, gpre, w_mix_in, conv_w, conv_b, w_gate, b_r, b_i, lam)


def _heads_kernel(x_ref, mod_ref, gpre_ref, gpost_ref, win_ref, y_ref, scw_ref, scb_ref,
                  cmw_ref, cmb_ref, wout_ref, o_ref, *, layer, period):
    row = layer * N_SUB + 1
    pos = lax.broadcasted_iota(jnp.int32, (SUBT, SC_WIDTH), 0) & (period - 1)
    lane_head = (lax.broadcasted_iota(jnp.int32, (CHUNK, CM_WIDTH), 1)
                 >> (HEAD_DIM.bit_length() - 1))
    for s in range(ROW_TILE // SUBT):
        rows = pl.ds(s * SUBT, SUBT)
        x = x_ref[rows, :]
        h = _rms(x, gpre_ref[row:row + 1, :]) * (1.0 + mod_ref[0, 4:5, :]) + mod_ref[0, 3:4, :]
        p = _dot(h.astype(BF16), win_ref[...])

        o_lru = _gelu(p[:, 0:LRU_WIDTH]) * y_ref[rows, :]

        c0 = LRU_WIDTH
        sc_bg = p[:, c0:c0 + SC_WIDTH]
        z = p[:, c0 + SC_WIDTH:c0 + 2 * SC_WIDTH] * p[:, c0 + 2 * SC_WIDTH:c0 + 3 * SC_WIDTH]
        z_prev = jnp.where(pos != 0, pltpu.roll(z, 1, axis=0), 0.0)
        z_next = jnp.where(pos != period - 1, pltpu.roll(z, SUBT - 1, axis=0), 0.0)
        conv = (scb_ref[layer:layer + 1, :] + scw_ref[layer, 0:1, :] * z_prev
                + scw_ref[layer, 1:2, :] * z + scw_ref[layer, 2:3, :] * z_next)
        o_sc = sc_bg * conv

        c1 = LRU_WIDTH + 3 * SC_WIDTH
        gu = _gelu(p[:, c1:c1 + CM_WIDTH])
        gv = _gelu(p[:, c1 + CM_WIDTH:c1 + 2 * CM_WIDTH])
        o_cm = []
        for n in range(SUBT // CHUNK):
            gvn = gv[n * CHUNK:(n + 1) * CHUNK, :]
            stacked = jnp.concatenate(
                [jnp.where(lane_head == hd, gvn, 0.0) for hd in range(CM_HEADS)], axis=0)
            sg = _dot(cmw_ref[...], stacked.astype(BF16)) + cmb_ref[...]
            o_cm.append(gu[n * CHUNK:(n + 1) * CHUNK, :] * sg)
        o_cm = jnp.concatenate(o_cm, axis=0)

        mix = (_dot(o_lru.astype(BF16), wout_ref[0:LRU_WIDTH, :])
               + _dot(o_sc.astype(BF16), wout_ref[LRU_WIDTH:LRU_WIDTH + SC_WIDTH, :])
               + _dot(o_cm.astype(BF16), wout_ref[LRU_WIDTH + SC_WIDTH:, :]))
        o_ref[rows, :] = x + mod_ref[0, 5:6, :] * _rms(mix, gpost_ref[row:row + 1, :])


def _heads(x, mod, gpre, gpost, w_mix_in, y, sc_w, sc_b, cm_w, cm_b, w_mix_out, *, layer,
           rows_per_batch):
    n_rows = x.shape[0]
    period = CTX_LEN if rows_per_batch is None else GRID_W
    kern = functools.partial(_heads_kernel, layer=layer, period=period)
    return pl.pallas_call(
        kern,
        grid=(n_rows // ROW_TILE,),
        in_specs=[
            pl.BlockSpec((ROW_TILE, D_MODEL), lambda i: (i, 0)),
            _mod_spec(layer, rows_per_batch),
            _full((DEPTH * N_SUB, D_MODEL)),
            _full((DEPTH * N_SUB, D_MODEL)),
            pl.BlockSpec((None, D_MODEL, D_REST), lambda i: (layer, 0, 0)),
            pl.BlockSpec((ROW_TILE, LRU_WIDTH), lambda i: (i, 0)),
            _full((DEPTH, 3, SC_WIDTH)),
            _full((DEPTH, SC_WIDTH)),
            pl.BlockSpec((None, CHUNK, CM_HEADS * CHUNK), lambda i: (layer, 0, 0)),
            pl.BlockSpec((None, CHUNK, CM_WIDTH), lambda i: (layer, 0, 0)),
            pl.BlockSpec((None, D_MODEL, D_MODEL), lambda i: (layer, 0, 0)),
        ],
        out_specs=pl.BlockSpec((ROW_TILE, D_MODEL), lambda i: (i, 0)),
        out_shape=jax.ShapeDtypeStruct((n_rows, D_MODEL), F32),
        compiler_params=pltpu.CompilerParams(
            dimension_semantics=("arbitrary",), vmem_limit_bytes=VMEM_LIMIT),
        name=f"heads_l{layer}_{'ctx' if rows_per_batch is None else 'lat'}",
    )(x, mod, gpre, gpost, w_mix_in, y, sc_w, sc_b, cm_w, cm_b, w_mix_out)


def _gate_weights(w_r, w_i):
    heads_per_half = (LRU_WIDTH // 2) // HEAD_DIM
    eye = jnp.eye(heads_per_half, dtype=F32)

    def block_diag(w):
        w = w.reshape(DEPTH, 2, 2, heads_per_half, HEAD_DIM, HEAD_DIM)
        bd = jnp.einsum('ldpgij,gk->ldpgikj', w, eye)
        return bd.reshape(DEPTH, 2, 2, LRU_WIDTH // 2, LRU_WIDTH // 2)

    r, i = block_diag(w_r), block_diag(w_i)
    cols = [r[:, 0], i[:, 0], r[:, 1], i[:, 1]]
    return jnp.concatenate(cols, axis=-1).astype(BF16)


def kernel(x, c, ctx, c_ctx, w_mod, b_mod, norm_pre, norm_post, ffn_w_in, ffn_w_out, w_mix_in,
           w_mix_out, lru_conv_w, lru_conv_b, lru_w_r, lru_b_r, lru_w_i, lru_b_i, lru_lambda,
           sc_conv_w, sc_conv_b, cm_w_s, cm_b_s):
    c_all = jnp.concatenate(
        [c, c_ctx[None, :], jnp.zeros((MOD_ROWS - BATCH - 1, D_MODEL), F32)], axis=0)
    mod = _modulation(c_all, w_mod, b_mod).reshape(DEPTH * MOD_ROWS, N_MOD, D_MODEL)

    gpre = norm_pre.reshape(DEPTH * N_SUB, D_MODEL)
    gpost = norm_post.reshape(DEPTH * N_SUB, D_MODEL)
    w_in = ffn_w_in.astype(BF16)
    w_out = ffn_w_out.astype(BF16)
    w_mi_lru = w_mix_in[:, :, :LRU_WIDTH].astype(BF16)
    w_mi_rest = w_mix_in[:, :, LRU_WIDTH:].astype(BF16)
    w_mo = w_mix_out.astype(BF16)
    w_gate = _gate_weights(lru_w_r, lru_w_i)
    cm_w = cm_w_s.transpose(0, 2, 1, 3).reshape(DEPTH, CHUNK, CM_HEADS * CHUNK).astype(BF16)
    cm_b = jnp.repeat(cm_b_s.transpose(0, 2, 1), HEAD_DIM, axis=2)

    xl = x.reshape(BATCH * SEQ, D_MODEL)
    xc = ctx.reshape(BATCH * CTX_LEN, D_MODEL)
    streams = {SEQ: xl, None: xc}
    for layer in range(DEPTH):
        last = layer == DEPTH - 1
        ffn = functools.partial(_ffn, mod=mod, gpre=gpre, gpost=gpost, w_in=w_in, w_out=w_out,
                                layer=layer)
        heads = functools.partial(_heads, mod=mod, gpre=gpre, gpost=gpost, w_mix_in=w_mi_rest,
                                  sc_w=sc_conv_w, sc_b=sc_conv_b, cm_w=cm_w, cm_b=cm_b,
                                  w_mix_out=w_mo, layer=layer)
        for key in streams:
            streams[key] = ffn(streams[key], sub=0, which=0, rows_per_batch=key)
        yl, yc = _lru(streams[SEQ], streams[None], mod, gpre, w_mi_lru, lru_conv_w, lru_conv_b,
                      w_gate, lru_b_r, lru_b_i, lru_lambda, layer=layer)
        if last:
            del streams[None]
        ys = {SEQ: yl, None: yc}
        for key in streams:
            streams[key] = heads(streams[key], y=ys[key], rows_per_batch=key)
            streams[key] = ffn(streams[key], sub=2, which=1, rows_per_batch=key)
    return streams[SEQ].reshape(BATCH, SEQ, D_MODEL)
```
